```python
import jax, jax.numpy as jnp
from jax import lax
import numpy as np

D_MODEL = 4096
BATCH = 1
SEQ = 8192
DEPTH = 4

CHUNK = 64
Q_BLOCK = 128
HEAD_DIM = 128
D_MIX = D_MODEL
D_SBA = D_MIX // 2
D_HGRN = D_MIX - D_SBA
N_SBA_HEADS = D_SBA // HEAD_DIM
N_HGRN_HEADS = D_HGRN // HEAD_DIM
D_IN = 3 * D_SBA + 4 * D_HGRN
SPLITS = [D_SBA, 2 * D_SBA, 3 * D_SBA, 3 * D_SBA + D_HGRN,
          3 * D_SBA + 2 * D_HGRN, 3 * D_SBA + 3 * D_HGRN]
D_FF = 11008
CONV_WIDTH = 3
EPS = 1e-6

kernel_name = "hymba_sba_hgrn2_convffn_sandwich"


def rms_norm(x, gain):
    xf = x.astype(jnp.float32)
    var = jnp.mean(xf * xf, axis=-1, keepdims=True)
    return (xf * lax.rsqrt(var + EPS) * gain.astype(jnp.float32)).astype(x.dtype)


def to_heads(t, n_heads):
    b, s, _ = t.shape
    return t.reshape(b, s, n_heads, HEAD_DIM).transpose(0, 2, 1, 3)


def from_heads(t):
    b, h, s, d = t.shape
    return t.transpose(0, 2, 1, 3).reshape(b, s, h * d)


def stick_breaking_attention(q, k, v):
    _, _, s_len, hd = q.shape
    scale = hd ** -0.5
    outs = []
    for blk in range(s_len // Q_BLOCK):
        start = blk * Q_BLOCK
        end = start + Q_BLOCK
        qb = q[:, :, start:end]
        kb = k[:, :, :end]
        vb = v[:, :, :end]
        z = jnp.einsum("bhtd,bhsd->bhts", qb, kb) * scale
        t_idx = start + jnp.arange(Q_BLOCK)[:, None]
        s_idx = jnp.arange(end)[None, :]
        strict = s_idx < t_idx
        log_keep = jnp.where(strict, jax.nn.log_sigmoid(-z), 0.0)
        suffix = lax.cumsum(log_keep, axis=3, reverse=True) - log_keep
        log_w = jnp.where(strict, jax.nn.log_sigmoid(z) + suffix, -jnp.inf)
        w = jnp.exp(log_w)
        outs.append(jnp.einsum("bhts,bhsd->bhtd", w, vb))
    return jnp.concatenate(outs, axis=2)


def hgrn2_chunkwise(q, log_f, k, v):
    b, h, s_len, dk = q.shape
    dv = v.shape[-1]
    nc = s_len // CHUNK

    def to_chunks(t):
        return jnp.moveaxis(t.reshape(b, h, nc, CHUNK, t.shape[-1]), 2, 0)

    causal = jnp.tril(jnp.ones((CHUNK, CHUNK), dtype=bool))[None, None, :, :, None]

    def step(state, inp):
        qi, gi, ki, vi = inp
        g_cum = jnp.cumsum(gi, axis=2)
        rel = jnp.where(causal, g_cum[:, :, :, None, :] - g_cum[:, :, None, :, :], -jnp.inf)
        decay = jnp.exp(rel)
        scores = jnp.sum(qi[:, :, :, None, :] * ki[:, :, None, :, :] * decay, axis=-1)
        intra = jnp.einsum("bhts,bhsv->bhtv", scores, vi)
        inter = jnp.einsum("bhtd,bhdv->bhtv", qi * jnp.exp(g_cum), state)
        g_last = g_cum[:, :, -1:, :]
        new_state = (jnp.exp(g_last[:, :, 0, :])[..., None] * state
                     + jnp.einsum("bhsd,bhsv->bhdv", ki * jnp.exp(g_last - g_cum), vi))
        return new_state, intra + inter

    state0 = jnp.zeros((b, h, dk, dv), jnp.float32)
    _, out = lax.scan(step, state0, (to_chunks(q), to_chunks(log_f), to_chunks(k), to_chunks(v)))
    return jnp.moveaxis(out, 0, 2).reshape(b, h, s_len, dv)


def causal_depthwise_conv(u, w, bias):
    s_len = u.shape[1]
    u_pad = jnp.pad(u, ((0, 0), (CONV_WIDTH - 1, 0), (0, 0)))
    out = bias + w[0] * u_pad[:, 0:s_len]
    for j in range(1, CONV_WIDTH):
        out = out + w[j] * u_pad[:, j:j + s_len]
    return out


def setup_inputs(seed: int = 0) -> dict:
    key = jax.random.key(seed)
    ks = jax.random.split(key, 16)
    f32 = jnp.float32
    x = jax.random.normal(ks[0], (BATCH, SEQ, D_MODEL), f32)
    w_in = jax.random.normal(ks[1], (DEPTH, D_MODEL, D_IN), f32) * D_MODEL ** -0.5
    sba_norm = 1.0 + 0.05 * jax.random.normal(ks[2], (DEPTH, D_SBA), f32)
    hgrn_norm = 1.0 + 0.05 * jax.random.normal(ks[3], (DEPTH, D_HGRN), f32)
    lb_logits = 0.5 * jax.random.normal(ks[4], (DEPTH, D_HGRN), f32)
    w_out = jax.random.normal(ks[5], (DEPTH, D_MIX, D_MODEL), f32) * D_MIX ** -0.5
    pre_mix_norm = 1.0 + 0.05 * jax.random.normal(ks[6], (DEPTH, D_MODEL), f32)
    post_mix_norm = 1.0 + 0.05 * jax.random.normal(ks[7], (DEPTH, D_MODEL), f32)
    pre_ffn_norm = 1.0 + 0.05 * jax.random.normal(ks[8], (DEPTH, D_MODEL), f32)
    post_ffn_norm = 1.0 + 0.05 * jax.random.normal(ks[9], (DEPTH, D_MODEL), f32)
    w_up = jax.random.normal(ks[10], (DEPTH, D_MODEL, 2 * D_FF), f32) * D_MODEL ** -0.5
    conv_w = jax.random.normal(ks[11], (DEPTH, CONV_WIDTH, 2 * D_FF), f32) * CONV_WIDTH ** -0.5
    conv_b = 0.01 * jax.random.normal(ks[12], (DEPTH, 2 * D_FF), f32)
    w_down = jax.random.normal(ks[13], (DEPTH, D_FF, D_MODEL), f32) * D_FF ** -0.5
    return {"x": x, "w_in": w_in, "sba_norm": sba_norm, "hgrn_norm": hgrn_norm,
            "lb_logits": lb_logits, "w_out": w_out, "pre_mix_norm": pre_mix_norm,
            "post_mix_norm": post_mix_norm, "pre_ffn_norm": pre_ffn_norm,
            "post_ffn_norm": post_ffn_norm, "w_up": w_up, "conv_w": conv_w,
            "conv_b": conv_b, "w_down": w_down}


def reference(x, w_in, sba_norm, hgrn_norm, lb_logits, w_out, pre_mix_norm, post_mix_norm,
              pre_ffn_norm, post_ffn_norm, w_up, conv_w, conv_b, w_down):
    f32 = jnp.float32
    cum = jnp.cumsum(jax.nn.softmax(lb_logits.astype(f32), axis=0), axis=0)
    lower_bounds = cum - cum[0:1]

    for l in range(DEPTH):
        h = rms_norm(x, pre_mix_norm[l])
        proj = jnp.einsum("bsd,de->bse", h, w_in[l])
        q_a, k_a, v_a, q_b, f_b, i_b, g_b = jnp.split(proj, SPLITS, axis=-1)

        o_a = stick_breaking_attention(to_heads(q_a.astype(f32), N_SBA_HEADS),
                                       to_heads(k_a.astype(f32), N_SBA_HEADS),
                                       to_heads(v_a.astype(f32), N_SBA_HEADS))
        o_a = rms_norm(from_heads(o_a), sba_norm[l])

        lb = lower_bounds[l]
        zf = f_b.astype(f32)
        log_f = jnp.logaddexp(jnp.log(lb), jnp.log1p(-lb) + jax.nn.log_sigmoid(zf))
        k_in = (1.0 - lb) * jax.nn.sigmoid(-zf)
        q_hg = jax.nn.silu(q_b.astype(f32))
        o_b = hgrn2_chunkwise(to_heads(q_hg, N_HGRN_HEADS), to_heads(log_f, N_HGRN_HEADS),
                              to_heads(k_in, N_HGRN_HEADS),
                              to_heads(i_b.astype(f32), N_HGRN_HEADS))
        var_b = jnp.mean(o_b * o_b, axis=-1, keepdims=True)
        gain_b = hgrn_norm[l].astype(f32).reshape(N_HGRN_HEADS, 1, HEAD_DIM)
        o_b = from_heads(o_b * lax.rsqrt(var_b + EPS) * gain_b) * jax.nn.silu(g_b.astype(f32))

        mixed = jnp.concatenate([o_a.astype(f32), o_b], axis=-1).astype(x.dtype)
        y = jnp.einsum("bse,ed->bsd", mixed, w_out[l])
        x = x + rms_norm(y, post_mix_norm[l])

        h = rms_norm(x, pre_ffn_norm[l])
        u = jnp.einsum("bsd,df->bsf", h, w_up[l])
        u = causal_depthwise_conv(u, conv_w[l], conv_b[l])
        gate, up = jnp.split(u, 2, axis=-1)
        y = jnp.einsum("bsf,fd->bsd", jax.nn.silu(gate) * up, w_down[l])
        x = x + rms_norm(y, post_ffn_norm[l])
    return x
```

```python
import functools

import numpy as np
import jax
import jax.numpy as jnp
from jax import lax
from jax.experimental import pallas as pl
from jax.experimental.pallas import tpu as pltpu

F32 = jnp.float32
BF16 = jnp.bfloat16

HEAD_DIM = 128
EPS = 1e-6
CONV_WIDTH = 3
LANE = 128
V7X_VMEM_LIMIT = 56 * 1024 * 1024

SBA_SKIP_LOG = -110.0


def _params(sem, vmem=V7X_VMEM_LIMIT):
    return pltpu.CompilerParams(dimension_semantics=sem, vmem_limit_bytes=vmem)


def _rms(x, gain):
    var = jnp.mean(x * x, axis=-1, keepdims=True)
    return x * lax.rsqrt(var + EPS) * gain


def _norm_cast_kernel(x_ref, g_ref, o_ref):
    o_ref[...] = _rms(x_ref[...], g_ref[...]).astype(o_ref.dtype)


def norm_cast(x, gain, tm=256):
    s, d = x.shape
    return pl.pallas_call(
        _norm_cast_kernel,
        grid=(s // tm,),
        in_specs=[pl.BlockSpec((tm, d), lambda i: (i, 0)),
                  pl.BlockSpec((1, d), lambda i: (0, 0))],
        out_specs=pl.BlockSpec((tm, d), lambda i: (i, 0)),
        out_shape=jax.ShapeDtypeStruct((s, d), BF16),
        compiler_params=_params(("parallel",)),
        name="norm_cast",
    )(x, gain.reshape(1, d))


def _resid_norm_kernel(x_ref, y_ref, gpost_ref, gnext_ref, xo_ref, ho_ref):
    xn = x_ref[...] + _rms(y_ref[...], gpost_ref[...])
    xo_ref[...] = xn
    ho_ref[...] = _rms(xn, gnext_ref[...]).astype(ho_ref.dtype)


def resid_norm(x, y, g_post, g_next, tm=256):
    s, d = x.shape
    row = pl.BlockSpec((tm, d), lambda i: (i, 0))
    vec = pl.BlockSpec((1, d), lambda i: (0, 0))
    return pl.pallas_call(
        _resid_norm_kernel,
        grid=(s // tm,),
        in_specs=[row, row, vec, vec],
        out_specs=[row, row],
        out_shape=[jax.ShapeDtypeStruct((s, d), F32), jax.ShapeDtypeStruct((s, d), BF16)],
        compiler_params=_params(("parallel",)),
        name="resid_norm",
    )(x, y, g_post.reshape(1, d), g_next.reshape(1, d))


def _mm_kernel(a_ref, w_ref, o_ref):
    o_ref[...] = jnp.dot(a_ref[...], w_ref[...],
                         preferred_element_type=F32).astype(o_ref.dtype)


def matmul(a, w, n_off, n, out_dtype, tm=1024, tn=1024):
    m, k = a.shape
    joff = n_off // tn
    return pl.pallas_call(
        _mm_kernel,
        grid=(n // tn, m // tm),
        in_specs=[pl.BlockSpec((tm, k), lambda j, i: (i, 0)),
                  pl.BlockSpec((k, tn), lambda j, i: (0, j + joff))],
        out_specs=pl.BlockSpec((tm, tn), lambda j, i: (i, j)),
        out_shape=jax.ShapeDtypeStruct((m, n), out_dtype),
        compiler_params=_params(("parallel", "parallel")),
        name="matmul",
    )(a, w)


def _mm2_kernel(a_ref, b_ref, wa_ref, wb_ref, o_ref):
    acc = jnp.dot(a_ref[...], wa_ref[...], preferred_element_type=F32)
    acc += jnp.dot(b_ref[...], wb_ref[...], preferred_element_type=F32)
    o_ref[...] = acc.astype(o_ref.dtype)


def matmul_split_k(a, b, w, tm=1024, tn=1024):
    m, kh = a.shape
    n = w.shape[1]
    return pl.pallas_call(
        _mm2_kernel,
        grid=(n // tn, m // tm),
        in_specs=[pl.BlockSpec((tm, kh), lambda j, i: (i, 0)),
                  pl.BlockSpec((tm, kh), lambda j, i: (i, 0)),
                  pl.BlockSpec((kh, tn), lambda j, i: (0, j)),
                  pl.BlockSpec((kh, tn), lambda j, i: (1, j))],
        out_specs=pl.BlockSpec((tm, tn), lambda j, i: (i, j)),
        out_shape=jax.ShapeDtypeStruct((m, n), F32),
        compiler_params=_params(("parallel", "parallel")),
        name="matmul_split_k",
    )(a, b, w, w)


def _mmk_kernel(a_ref, w_ref, o_ref):
    @pl.when(pl.program_id(2) == 0)
    def _():
        o_ref[...] = jnp.zeros_like(o_ref)
    o_ref[...] += jnp.dot(a_ref[...], w_ref[...], preferred_element_type=F32)


def matmul_kgrid(a, w, tm=1024, tn=512, nk=2):
    m, k = a.shape
    n = w.shape[1]
    tk = k // nk
    return pl.pallas_call(
        _mmk_kernel,
        grid=(n // tn, m // tm, nk),
        in_specs=[pl.BlockSpec((tm, tk), lambda j, i, kk: (i, kk)),
                  pl.BlockSpec((tk, tn), lambda j, i, kk: (kk, j))],
        out_specs=pl.BlockSpec((tm, tn), lambda j, i, kk: (i, j)),
        out_shape=jax.ShapeDtypeStruct((m, n), F32),
        compiler_params=_params(("parallel", "parallel", "arbitrary")),
        name="matmul_kgrid",
    )(a, w)


def _softplus_parts(z):
    l = jnp.log1p(jnp.exp(-jnp.abs(z)))
    ls = jnp.minimum(z, 0.0) - l
    return ls, ls - z


def _split_bf16(x):
    hi = x.astype(BF16)
    lo = (x - hi.astype(F32)).astype(BF16)
    return hi, lo


def _sba_kernel(q_ref, k_ref, v_ref, uu_ref, o_ref, *, tq, scale):
    s_len = q_ref.shape[0]
    tk = LANE
    row = lax.broadcasted_iota(jnp.int32, (tq, tk), 0)
    col = lax.broadcasted_iota(jnp.int32, (tq, tk), 1)
    diff = col - row
    nn_t = (((1,), (1,)), ((), ()))

    def q_tile(i, _):
        r0 = pl.multiple_of(i * tq, tq)
        q = q_ref[pl.ds(r0, tq), :]

        def cond(st):
            j, cmax, _, _ = st
            return jnp.logical_and(j >= 0, cmax > SBA_SKIP_LOG)

        def body(st):
            j, _, carry, acc = st
            c0 = pl.multiple_of(j * tk, tk)
            k = k_ref[pl.ds(c0, tk), :]
            v = v_ref[pl.ds(c0, tk), :]
            z = lax.dot_general(q, k, nn_t, preferred_element_type=F32) * scale
            strict = diff < (r0 - c0)
            ls, lk = _softplus_parts(z)
            lk = jnp.where(strict, lk, 0.0)
            hi, lo = _split_bf16(lk)
            cs = jnp.dot(jnp.concatenate([hi, lo], axis=1), uu_ref[...],
                         preferred_element_type=F32)
            w = jnp.where(strict, jnp.exp(ls + cs[:, :tk] + carry), 0.0)
            acc = acc + jnp.dot(w.astype(BF16), v, preferred_element_type=F32)
            carry = carry + cs[:, tk:]
            return j - 1, jnp.max(carry), carry, acc

        j0 = ((i + 1) * tq) // tk - 1
        zeros = jnp.zeros((tq, tk), F32)
        _, _, _, acc = lax.while_loop(cond, body, (j0, jnp.float32(0.0), zeros, zeros))
        o_ref[pl.ds(r0, tq), :] = acc
        return 0

    lax.fori_loop(0, s_len // tq, q_tile, 0)


def _suffix_matrix():
    t = LANE
    j = np.arange(t)[:, None]
    s = np.arange(t)[None, :]
    u = (j > s).astype(np.float32)
    half = np.concatenate([u, np.ones((t, t), np.float32)], axis=1)
    return jnp.asarray(np.concatenate([half, half], axis=0), dtype=BF16)


def sba_attention(proj_a, n_heads, tq=128):
    s_len = proj_a.shape[0]
    blk = lambda off: pl.BlockSpec((s_len, HEAD_DIM), lambda h: (0, h + off))
    kern = functools.partial(_sba_kernel, tq=tq, scale=HEAD_DIM ** -0.5)
    return pl.pallas_call(
        kern,
        grid=(n_heads,),
        in_specs=[blk(0), blk(n_heads), blk(2 * n_heads),
                  pl.BlockSpec((2 * LANE, 2 * LANE), lambda h: (0, 0))],
        out_specs=pl.BlockSpec((s_len, HEAD_DIM), lambda h: (0, h)),
        out_shape=jax.ShapeDtypeStruct((s_len, n_heads * HEAD_DIM), F32),
        compiler_params=_params(("parallel",)),
        name="sba_attention",
    )(proj_a, proj_a, proj_a, _suffix_matrix())


HGRN_CHUNK = 128


def _hgrn_constants(c):
    t = np.arange(c)[:, None]
    r = np.arange(c)[None, :]
    blocks = [(r <= t), (r > t)]
    m = c // 2
    while m >= 1:
        same = (t // m) == (r // m)
        bit = (t & m) != 0
        blocks.append(np.where(bit, same & (r <= t), same & (r > t)))
        m //= 2
    stack = np.concatenate(blocks, axis=0).astype(np.float32)
    x = t ^ r
    lev = np.full((c, c), -2, np.int32)
    lev[np.arange(c), np.arange(c)] = -1
    nlev = len(blocks) - 2
    for i in range(nlev):
        m = c >> (i + 1)
        lev[(r < t) & (x >= m) & (x < 2 * m)] = i
    return jnp.asarray(stack, dtype=BF16), jnp.asarray(lev), nlev


def _hgrn_kernel(q_ref, f_ref, i_ref, g_ref, lbl_ref, gain_ref, stack_ref, lev_ref,
                 o_ref, st_ref, *, layer, c, nlev):
    s_len = q_ref.shape[0]
    nn_t = (((1,), (1,)), ((), ()))
    tn_n = (((0,), (0,)), ((), ()))

    if layer > 0:
        logits = lbl_ref[...]
        e = jnp.exp(logits - jnp.max(logits, axis=0, keepdims=True))
        p = e / jnp.sum(e, axis=0, keepdims=True)
        lb = p[1:2, :]
        for r in range(2, layer + 1):
            lb = lb + p[r:r + 1, :]
        log_lb = jnp.log(lb)
        log_1m = jnp.log1p(-lb)
        one_m = 1.0 - lb
    gain = gain_ref[...]
    st_ref[...] = jnp.zeros_like(st_ref)

    def chunk(ci, _):
        r0 = pl.multiple_of(ci * c, c)
        rows = pl.ds(r0, c)
        zf = f_ref[rows, :]
        e = jnp.exp(-jnp.abs(zf))
        ls = jnp.minimum(zf, 0.0) - jnp.log1p(e)
        sig_neg = jnp.where(zf >= 0.0, e, 1.0) / (1.0 + e)
        if layer > 0:
            b = log_1m + ls
            log_f = jnp.maximum(log_lb, b) + jnp.log1p(jnp.exp(-jnp.abs(log_lb - b)))
            k_in = one_m * sig_neg
        else:
            log_f = ls
            k_in = sig_neg
        qb = q_ref[rows, :]
        q_in = qb * (1.0 / (1.0 + jnp.exp(-qb)))
        v = i_ref[rows, :].astype(BF16)

        hi, lo = _split_bf16(log_f)
        ex = jnp.dot(stack_ref[...], jnp.concatenate([hi, lo], axis=1),
                     preferred_element_type=F32)
        ex = ex[:, :HEAD_DIM] + ex[:, HEAD_DIM:]
        dec = jnp.exp(ex)

        q_b = q_in.astype(BF16)
        k_b = k_in.astype(BF16)
        lev = lev_ref[...]
        a = jnp.where(lev == -1, lax.dot_general(q_b, k_b, nn_t, preferred_element_type=F32), 0.0)
        for i in range(nlev):
            d = dec[(2 + i) * c:(3 + i) * c, :]
            a_i = lax.dot_general((q_in * d).astype(BF16), (k_in * d).astype(BF16), nn_t,
                                  preferred_element_type=F32)
            a = a + jnp.where(lev == i, a_i, 0.0)

        st = st_ref[...]
        q_full = (q_in * dec[0:c, :]).astype(BF16)
        k_full = (k_in * dec[c:2 * c, :]).astype(BF16)
        o = jnp.dot(a.astype(BF16), v, preferred_element_type=F32)
        o = o + lax.dot_general(q_full, st.astype(BF16), nn_t, preferred_element_type=F32)
        g_last = ex[c - 1:c, :]
        st_ref[...] = st * jnp.exp(g_last) + lax.dot_general(
            v, k_full, tn_n, preferred_element_type=F32)

        var = jnp.mean(o * o, axis=-1, keepdims=True)
        gb = g_ref[rows, :]
        gate = gb * (1.0 / (1.0 + jnp.exp(-gb)))
        o_ref[rows, :] = (o * lax.rsqrt(var + EPS) * gain * gate).astype(o_ref.dtype)
        return 0

    lax.fori_loop(0, s_len // c, chunk, 0)


def hgrn2(proj_b, lb_logits, gain, layer, n_heads):
    s_len = proj_b.shape[0]
    depth = lb_logits.shape[0]
    c = HGRN_CHUNK
    stack, lev, nlev = _hgrn_constants(c)
    blk = lambda off: pl.BlockSpec((s_len, HEAD_DIM), lambda h: (0, h + off))
    kern = functools.partial(_hgrn_kernel, layer=layer, c=c, nlev=nlev)
    return pl.pallas_call(
        kern,
        grid=(n_heads,),
        in_specs=[blk(0), blk(n_heads), blk(2 * n_heads), blk(3 * n_heads),
                  pl.BlockSpec((depth, HEAD_DIM), lambda h: (0, h)),
                  pl.BlockSpec((1, HEAD_DIM), lambda h: (0, h)),
                  pl.BlockSpec(stack.shape, lambda h: (0, 0)),
                  pl.BlockSpec(lev.shape, lambda h: (0, 0))],
        out_specs=pl.BlockSpec((s_len, HEAD_DIM), lambda h: (0, h)),
        out_shape=jax.ShapeDtypeStruct((s_len, n_heads * HEAD_DIM), BF16),
        scratch_shapes=[pltpu.VMEM((HEAD_DIM, HEAD_DIM), F32)],
        compiler_params=_params(("parallel",)),
        name="hgrn2",
    )(proj_b, proj_b, proj_b, proj_b, lb_logits, gain.reshape(1, -1), stack, lev)


def _conv_gate_kernel(ug_ref, uu_ref, wg_ref, wu_ref, bg_ref, bu_ref, o_ref, *, rows):
    s_len = ug_ref.shape[0]
    ridx = lax.broadcasted_iota(jnp.int32, (rows, LANE), 0)

    def conv(u_ref, w_ref, b_ref, ci, r0):
        cur = u_ref[pl.ds(r0, rows), :]
        prev0 = pl.multiple_of(jnp.maximum(r0 - 8, 0), 8)
        tail = jnp.where(ci > 0, u_ref[pl.ds(prev0, 8), :], 0.0)
        p1 = tail[7:8, :]
        p2 = tail[6:7, :]
        m1 = jnp.where(ridx == 0, p1, pltpu.roll(cur, 1, 0))
        m2 = jnp.where(ridx == 0, p2, jnp.where(ridx == 1, p1, pltpu.roll(cur, 2, 0)))
        w = w_ref[...]
        out = b_ref[...] + w[0:1, :] * m2
        out = out + w[1:2, :] * m1
        return out + w[2:3, :] * cur

    def body(ci, _):
        r0 = pl.multiple_of(ci * rows, rows)
        g = conv(ug_ref, wg_ref, bg_ref, ci, r0)
        u = conv(uu_ref, wu_ref, bu_ref, ci, r0)
        act = g * (1.0 / (1.0 + jnp.exp(-g))) * u
        o_ref[pl.ds(r0, rows), :] = act.astype(o_ref.dtype)
        return 0

    lax.fori_loop(0, s_len // rows, body, 0)


def conv_gate(u, conv_w, conv_b, rows=256):
    s_len, f2 = u.shape
    f = f2 // 2
    nt = f // LANE
    col = lambda off: pl.BlockSpec((s_len, LANE), lambda j: (0, j + off))
    wsp = lambda off: pl.BlockSpec((CONV_WIDTH, LANE), lambda j: (0, j + off))
    bsp = lambda off: pl.BlockSpec((1, LANE), lambda j: (0, j + off))
    cb = conv_b.reshape(1, f2)
    return pl.pallas_call(
        functools.partial(_conv_gate_kernel, rows=rows),
        grid=(nt,),
        in_specs=[col(0), col(nt), wsp(0), wsp(nt), bsp(0), bsp(nt)],
        out_specs=pl.BlockSpec((s_len, LANE), lambda j: (0, j)),
        out_shape=jax.ShapeDtypeStruct((s_len, f), BF16),
        compiler_params=_params(("parallel",)),
        name="conv_gate",
    )(u, u, conv_w, conv_w, cb, cb)


def kernel(x, w_in, sba_norm, hgrn_norm, lb_logits, w_out, pre_mix_norm, post_mix_norm,
           pre_ffn_norm, post_ffn_norm, w_up, conv_w, conv_b, w_down):
    batch, s_len, d_model = x.shape
    assert batch == 1
    depth = w_in.shape[0]
    d_sba = sba_norm.shape[1]
    d_hgrn = hgrn_norm.shape[1]
    n_sba = d_sba // HEAD_DIM
    n_hgrn = d_hgrn // HEAD_DIM

    xs = x.reshape(s_len, d_model)
    h = norm_cast(xs, pre_mix_norm[0])
    for l in range(depth):
        w_in_b = w_in[l].astype(BF16)
        proj_a = matmul(h, w_in_b, 0, 3 * d_sba, BF16)
        proj_b = matmul(h, w_in_b, 3 * d_sba, 4 * d_hgrn, F32)

        o_a = sba_attention(proj_a, n_sba)
        mixed_a = norm_cast(o_a, sba_norm[l])
        mixed_b = hgrn2(proj_b, lb_logits, hgrn_norm[l], l, n_hgrn)

        y = matmul_split_k(mixed_a, mixed_b, w_out[l].astype(BF16))
        xs, h = resid_norm(xs, y, post_mix_norm[l], pre_ffn_norm[l])

        u = matmul(h, w_up[l].astype(BF16), 0, w_up.shape[2], F32, tn=512)
        act = conv_gate(u, conv_w[l], conv_b[l])
        y = matmul_kgrid(act, w_down[l].astype(BF16))
        g_next = pre_mix_norm[(l + 1) % depth]
        xs, h = resid_norm(xs, y, post_ffn_norm[l], g_next)
    return xs.reshape(batch, s_len, d_model)
```

```python
import functools

import numpy as np
import jax
import jax.numpy as jnp
from jax import lax
from jax.experimental import pallas as pl
from jax.experimental.pallas import tpu as pltpu

F32 = jnp.float32
BF16 = jnp.bfloat16

HEAD_DIM = 128
EPS = 1e-6
CONV_WIDTH = 3
LOG2E = 1.4426950408889634
LANE = 128
V7X_VMEM_LIMIT = 56 * 1024 * 1024

SBA_SKIP_LOG = -110.0


def _params(sem, vmem=V7X_VMEM_LIMIT):
    return pltpu.CompilerParams(dimension_semantics=sem, vmem_limit_bytes=vmem)


def _rms(x, gain):
    var = jnp.mean(x * x, axis=-1, keepdims=True)
    return x * lax.rsqrt(var + EPS) * gain


def _norm_cast_kernel(x_ref, g_ref, o_ref):
    o_ref[...] = _rms(x_ref[...], g_ref[...]).astype(o_ref.dtype)


def norm_cast(x, gain, tm=256):
    s, d = x.shape
    return pl.pallas_call(
        _norm_cast_kernel,
        grid=(s // tm,),
        in_specs=[pl.BlockSpec((tm, d), lambda i: (i, 0)),
                  pl.BlockSpec((1, d), lambda i: (0, 0))],
        out_specs=pl.BlockSpec((tm, d), lambda i: (i, 0)),
        out_shape=jax.ShapeDtypeStruct((s, d), BF16),
        compiler_params=_params(("parallel",)),
        name="norm_cast",
    )(x, gain.reshape(1, d))


def _resid_norm_kernel(x_ref, y_ref, gpost_ref, gnext_ref, xo_ref, ho_ref):
    xn = x_ref[...] + _rms(y_ref[...], gpost_ref[...])
    xo_ref[...] = xn
    ho_ref[...] = _rms(xn, gnext_ref[...]).astype(ho_ref.dtype)


def resid_norm(x, y, g_post, g_next, tm=256):
    s, d = x.shape
    row = pl.BlockSpec((tm, d), lambda i: (i, 0))
    vec = pl.BlockSpec((1, d), lambda i: (0, 0))
    return pl.pallas_call(
        _resid_norm_kernel,
        grid=(s // tm,),
        in_specs=[row, row, vec, vec],
        out_specs=[row, row],
        out_shape=[jax.ShapeDtypeStruct((s, d), F32), jax.ShapeDtypeStruct((s, d), BF16)],
        compiler_params=_params(("parallel",)),
        name="resid_norm",
    )(x, y, g_post.reshape(1, d), g_next.reshape(1, d))


def _mm_kernel(a_ref, w_ref, o_ref, wb_ref):
    @pl.when(pl.program_id(1) == 0)
    def _():
        wb_ref[...] = w_ref[...].astype(BF16)
    o_ref[...] = jnp.dot(a_ref[...], wb_ref[...],
                         preferred_element_type=F32).astype(o_ref.dtype)


def matmul(a, w, layer, n_off, n, out_dtype, tm=1024, tn=512):
    m, k = a.shape
    joff = n_off // tn
    return pl.pallas_call(
        _mm_kernel,
        grid=(n // tn, m // tm),
        in_specs=[pl.BlockSpec((tm, k), lambda j, i: (i, 0)),
                  pl.BlockSpec((None, k, tn), lambda j, i: (layer, 0, j + joff))],
        out_specs=pl.BlockSpec((tm, tn), lambda j, i: (i, j)),
        out_shape=jax.ShapeDtypeStruct((m, n), out_dtype),
        scratch_shapes=[pltpu.VMEM((k, tn), BF16)],
        compiler_params=_params(("parallel", "arbitrary")),
        name="matmul",
    )(a, w)


def _mm2_kernel(a_ref, b_ref, wa_ref, wb_ref, o_ref, wab_ref, wbb_ref):
    @pl.when(pl.program_id(1) == 0)
    def _():
        wab_ref[...] = wa_ref[...].astype(BF16)
        wbb_ref[...] = wb_ref[...].astype(BF16)
    acc = jnp.dot(a_ref[...], wab_ref[...], preferred_element_type=F32)
    acc += jnp.dot(b_ref[...], wbb_ref[...], preferred_element_type=F32)
    o_ref[...] = acc.astype(o_ref.dtype)


def matmul_split_k(a, b, w, layer, tm=1024, tn=512):
    m, kh = a.shape
    n = w.shape[2]
    return pl.pallas_call(
        _mm2_kernel,
        grid=(n // tn, m // tm),
        in_specs=[pl.BlockSpec((tm, kh), lambda j, i: (i, 0)),
                  pl.BlockSpec((tm, kh), lambda j, i: (i, 0)),
                  pl.BlockSpec((None, kh, tn), lambda j, i: (layer, 0, j)),
                  pl.BlockSpec((None, kh, tn), lambda j, i: (layer, 1, j))],
        out_specs=pl.BlockSpec((tm, tn), lambda j, i: (i, j)),
        out_shape=jax.ShapeDtypeStruct((m, n), F32),
        scratch_shapes=[pltpu.VMEM((kh, tn), BF16), pltpu.VMEM((kh, tn), BF16)],
        compiler_params=_params(("parallel", "arbitrary")),
        name="matmul_split_k",
    )(a, b, w, w)


def _mmk_kernel(a_ref, w_ref, o_ref):
    @pl.when(pl.program_id(2) == 0)
    def _():
        o_ref[...] = jnp.zeros_like(o_ref)
    o_ref[...] += jnp.dot(a_ref[...], w_ref[...], preferred_element_type=F32)


def matmul_kgrid(a, w, layer, tm=1024, tn=512, nk=2):
    m, k = a.shape
    n = w.shape[2]
    tk = k // nk
    return pl.pallas_call(
        _mmk_kernel,
        grid=(n // tn, m // tm, nk),
        in_specs=[pl.BlockSpec((tm, tk), lambda j, i, kk: (i, kk)),
                  pl.BlockSpec((None, tk, tn), lambda j, i, kk: (layer, kk, j))],
        out_specs=pl.BlockSpec((tm, tn), lambda j, i, kk: (i, j)),
        out_shape=jax.ShapeDtypeStruct((m, n), F32),
        compiler_params=_params(("parallel", "parallel", "arbitrary")),
        name="matmul_kgrid",
    )(a, w)


def _softplus_parts(z):
    l = jnp.log(1.0 + jnp.exp(-jnp.abs(z)))
    ls = jnp.minimum(z, 0.0) - l
    return ls, ls - z


def _split_bf16(x):
    hi = x.astype(BF16)
    lo = (x - hi.astype(F32)).astype(BF16)
    return hi, lo


SBA_BAND = 3
SBA_GROUP = 4


def _sba_kernel(q_ref, k_ref, v_ref, uu_ref, o_ref, carry_ref, *, scale):
    s_len = q_ref.shape[0]
    t = LANE
    n_groups = s_len // (t * SBA_GROUP)
    row = lax.broadcasted_iota(jnp.int32, (t, t), 0)
    col = lax.broadcasted_iota(jnp.int32, (t, t), 1)
    diff = col - row
    tri = diff < 0
    nn_t = (((1,), (1,)), ((), ()))

    def suffix_sums(lk):
        hi, lo = _split_bf16(lk)
        cs = jnp.dot(jnp.concatenate([hi, lo], axis=1), uu_ref[...],
                     preferred_element_type=F32)
        return cs[:, :t], cs[:, t:]

    def walk(i, j_start, carry):
        r0 = pl.multiple_of(i * t, t)
        q = q_ref[pl.ds(r0, t), :]

        def cond(st):
            j, cmax, _, _ = st
            return jnp.logical_and(j >= 0, cmax > SBA_SKIP_LOG)

        def body(st):
            j, _, carry, acc = st
            c0 = pl.multiple_of(j * t, t)
            k = k_ref[pl.ds(c0, t), :]
            v = v_ref[pl.ds(c0, t), :]
            z = lax.dot_general(q, k, nn_t, preferred_element_type=F32) * scale
            strict = diff < (r0 - c0)
            ls, lk = _softplus_parts(z)
            suf, tot = suffix_sums(jnp.where(strict, lk, 0.0))
            w = jnp.where(strict, jnp.exp(ls + suf + carry), 0.0)
            acc = acc + jnp.dot(w.astype(BF16), v, preferred_element_type=F32)
            carry = carry + tot
            return j - 1, jnp.max(carry), carry, acc

        init = (j_start, jnp.max(carry), carry, jnp.zeros((t, t), F32))
        return lax.while_loop(cond, body, init)[3]

    def band(i):
        r0 = pl.multiple_of(i * t, t)
        c0 = pl.multiple_of(r0 - (SBA_BAND - 1) * t, t)
        q = q_ref[pl.ds(r0, t), :]
        kb = k_ref[pl.ds(c0, SBA_BAND * t), :]
        vb = v_ref[pl.ds(c0, SBA_BAND * t), :]
        z = lax.dot_general(q, kb, nn_t, preferred_element_type=F32) * scale
        ls, lk = _softplus_parts(z)
        ws = []
        carry = None
        for b in reversed(range(SBA_BAND)):
            sl = slice(b * t, (b + 1) * t)
            diag = b == SBA_BAND - 1
            suf, tot = suffix_sums(jnp.where(tri, lk[:, sl], 0.0) if diag else lk[:, sl])
            lw = ls[:, sl] + suf
            if carry is not None:
                lw = lw + carry
            w = jnp.exp(lw)
            ws.append(jnp.where(tri, w, 0.0) if diag else w)
            carry = tot if carry is None else carry + tot
        w = jnp.concatenate(ws[::-1], axis=1).astype(BF16)
        return jnp.dot(w, vb, preferred_element_type=F32), carry

    for i in range(SBA_GROUP):
        o_ref[i * t:(i + 1) * t, :] = walk(i, i, jnp.zeros((t, t), F32))

    def group(g, _):
        cmax = None
        for u in range(SBA_GROUP):
            i = g * SBA_GROUP + u
            acc, carry = band(i)
            o_ref[pl.ds(pl.multiple_of(i * t, t), t), :] = acc
            carry_ref[u * t:(u + 1) * t, :] = carry
            cmax = carry if cmax is None else jnp.maximum(cmax, carry)

        @pl.when(jnp.max(cmax) > SBA_SKIP_LOG)
        def _():
            def far(u, _):
                i = g * SBA_GROUP + u
                rows = pl.ds(pl.multiple_of(i * t, t), t)
                carry = carry_ref[pl.ds(pl.multiple_of(u * t, t), t), :]
                o_ref[rows, :] += walk(i, i - SBA_BAND, carry)
                return 0
            lax.fori_loop(0, SBA_GROUP, far, 0)
        return 0

    lax.fori_loop(1, n_groups, group, 0)


def _suffix_matrix():
    t = LANE
    j = np.arange(t)[:, None]
    s = np.arange(t)[None, :]
    u = (j > s).astype(np.float32)
    half = np.concatenate([u, np.ones((t, t), np.float32)], axis=1)
    return jnp.asarray(np.concatenate([half, half], axis=0), dtype=BF16)


def sba_attention(proj_a, n_heads):
    s_len = proj_a.shape[0]
    assert SBA_GROUP >= SBA_BAND - 1 and s_len % (LANE * SBA_GROUP) == 0
    blk = lambda off: pl.BlockSpec((s_len, HEAD_DIM), lambda h: (0, h + off))
    kern = functools.partial(_sba_kernel, scale=HEAD_DIM ** -0.5)
    return pl.pallas_call(
        kern,
        grid=(n_heads,),
        in_specs=[blk(0), blk(n_heads), blk(2 * n_heads),
                  pl.BlockSpec((2 * LANE, 2 * LANE), lambda h: (0, 0))],
        out_specs=pl.BlockSpec((s_len, HEAD_DIM), lambda h: (0, h)),
        out_shape=jax.ShapeDtypeStruct((s_len, n_heads * HEAD_DIM), F32),
        scratch_shapes=[pltpu.VMEM((SBA_GROUP * LANE, LANE), F32)],
        compiler_params=_params(("parallel",)),
        name="sba_attention",
    )(proj_a, proj_a, proj_a, _suffix_matrix())


HGRN_CHUNK = 128
HGRN_UNROLL = 4


def _hgrn_constants(c):
    t = np.arange(c)[:, None]
    r = np.arange(c)[None, :]
    blocks = [(r <= t), (r > t)]
    m = c // 2
    while m >= 1:
        same = (t // m) == (r // m)
        bit = (t & m) != 0
        blocks.append(np.where(bit, same & (r <= t), same & (r > t)))
        m //= 2
    stack = np.concatenate(blocks, axis=0).astype(np.float32)
    x = t ^ r
    lev = np.full((c, c), -2, np.int32)
    lev[np.arange(c), np.arange(c)] = -1
    nlev = len(blocks) - 2
    for i in range(nlev):
        m = c >> (i + 1)
        lev[(r < t) & (x >= m) & (x < 2 * m)] = i
    return jnp.asarray(stack, dtype=BF16), jnp.asarray(lev), nlev


def _sigmoid(x):
    return 1.0 / (1.0 + jnp.exp(-x))


def _hgrn_kernel(q_ref, f_ref, i_ref, g_ref, lbl_ref, gain_ref, stack_ref, lev_ref,
                 o_ref, *, layer, c, nlev):
    s_len = q_ref.shape[0]
    nn_t = (((1,), (1,)), ((), ()))
    tn_n = (((0,), (0,)), ((), ()))

    if layer > 0:
        logits = lbl_ref[...]
        e = jnp.exp(logits - jnp.max(logits, axis=0, keepdims=True))
        p = e / jnp.sum(e, axis=0, keepdims=True)
        lb = p[1:2, :]
        for r in range(2, layer + 1):
            lb = lb + p[r:r + 1, :]
        log_lb = jnp.log(lb)
        log_1m = jnp.log1p(-lb)
        one_m = 1.0 - lb
    gain = gain_ref[...]

    def chunk(ci, st):
        r0 = pl.multiple_of(ci * c, c)
        rows = pl.ds(r0, c)
        zf = f_ref[rows, :]
        e = jnp.exp(-jnp.abs(zf))
        ls = jnp.minimum(zf, 0.0) - jnp.log(1.0 + e)
        sig_neg = jnp.where(zf >= 0.0, e, 1.0) / (1.0 + e)
        if layer > 0:
            b = log_1m + ls
            log_f = jnp.maximum(log_lb, b) + jnp.log(1.0 + jnp.exp(-jnp.abs(log_lb - b)))
            k_in = one_m * sig_neg
        else:
            log_f = ls
            k_in = sig_neg
        qb = q_ref[rows, :]
        q_in = qb * _sigmoid(qb)
        v = i_ref[rows, :].astype(BF16)

        hi, lo = _split_bf16(log_f * LOG2E)
        ex = jnp.dot(stack_ref[...], jnp.concatenate([hi, lo], axis=1),
                     preferred_element_type=F32)
        ex = ex[:, :HEAD_DIM] + ex[:, HEAD_DIM:]
        dec = jnp.exp2(ex).astype(BF16)

        lev = lev_ref[...]
        q_b = q_in.astype(BF16)
        k_b = k_in.astype(BF16)
        a = jnp.where(lev == -1, lax.dot_general(q_b, k_b, nn_t,
                                                 preferred_element_type=F32), 0.0)
        for i in range(nlev):
            d = dec[(2 + i) * c:(3 + i) * c, :]
            a_i = lax.dot_general(q_b * d, k_b * d, nn_t, preferred_element_type=F32)
            a = a + jnp.where(lev == i, a_i, 0.0)

        q_full = q_b * dec[0:c, :]
        k_full = k_b * dec[c:2 * c, :]
        o = jnp.dot(a.astype(BF16), v, preferred_element_type=F32)
        o = o + lax.dot_general(q_full, st.astype(BF16), nn_t, preferred_element_type=F32)
        g_last = ex[c - 1:c, :]
        st = st * jnp.exp2(g_last) + lax.dot_general(v, k_full, tn_n,
                                                     preferred_element_type=F32)

        var = jnp.mean(o * o, axis=-1, keepdims=True)
        gb = g_ref[rows, :]
        o_ref[rows, :] = (o * lax.rsqrt(var + EPS) * gain * (gb * _sigmoid(gb))
                          ).astype(o_ref.dtype)
        return st

    lax.fori_loop(0, s_len // c, chunk, jnp.zeros((HEAD_DIM, HEAD_DIM), F32),
                  unroll=HGRN_UNROLL)


def hgrn2(proj_b, lb_logits, gain, layer, n_heads):
    s_len = proj_b.shape[0]
    depth = lb_logits.shape[0]
    c = HGRN_CHUNK
    stack, lev, nlev = _hgrn_constants(c)
    blk = lambda off: pl.BlockSpec((s_len, HEAD_DIM), lambda h: (0, h + off))
    kern = functools.partial(_hgrn_kernel, layer=layer, c=c, nlev=nlev)
    return pl.pallas_call(
        kern,
        grid=(n_heads,),
        in_specs=[blk(0), blk(n_heads), blk(2 * n_heads), blk(3 * n_heads),
                  pl.BlockSpec((depth, HEAD_DIM), lambda h: (0, h)),
                  pl.BlockSpec((1, HEAD_DIM), lambda h: (0, h)),
                  pl.BlockSpec(stack.shape, lambda h: (0, 0)),
                  pl.BlockSpec(lev.shape, lambda h: (0, 0))],
        out_specs=pl.BlockSpec((s_len, HEAD_DIM), lambda h: (0, h)),
        out_shape=jax.ShapeDtypeStruct((s_len, n_heads * HEAD_DIM), BF16),
        compiler_params=_params(("parallel",)),
        name="hgrn2",
    )(proj_b, proj_b, proj_b, proj_b, lb_logits, gain.reshape(1, -1), stack, lev)


def _conv_gate_kernel(ug_ref, uu_ref, wg_ref, wu_ref, bg_ref, bu_ref, o_ref, *, rows):
    s_len = ug_ref.shape[0]
    ridx = lax.broadcasted_iota(jnp.int32, (rows, LANE), 0)

    def conv(u_ref, w_ref, b_ref, ci, r0):
        cur = u_ref[pl.ds(r0, rows), :]
        prev0 = pl.multiple_of(jnp.maximum(r0 - 8, 0), 8)
        tail = jnp.where(ci > 0, u_ref[pl.ds(prev0, 8), :], 0.0)
        p1 = tail[7:8, :]
        p2 = tail[6:7, :]
        m1 = jnp.where(ridx == 0, p1, pltpu.roll(cur, 1, 0))
        m2 = jnp.where(ridx == 0, p2, jnp.where(ridx == 1, p1, pltpu.roll(cur, 2, 0)))
        w = w_ref[...]
        out = b_ref[...] + w[0:1, :] * m2
        out = out + w[1:2, :] * m1
        return out + w[2:3, :] * cur

    def body(ci, _):
        r0 = pl.multiple_of(ci * rows, rows)
        g = conv(ug_ref, wg_ref, bg_ref, ci, r0)
        u = conv(uu_ref, wu_ref, bu_ref, ci, r0)
        o_ref[pl.ds(r0, rows), :] = (g * _sigmoid(g) * u).astype(o_ref.dtype)
        return 0

    lax.fori_loop(0, s_len // rows, body, 0)


def conv_gate(u, conv_w, conv_b, layer, rows=256):
    s_len, f2 = u.shape
    f = f2 // 2
    nt = f // LANE
    col = lambda off: pl.BlockSpec((s_len, LANE), lambda j: (0, j + off))
    wsp = lambda off: pl.BlockSpec((None, CONV_WIDTH, LANE), lambda j: (layer, 0, j + off))
    bsp = lambda off: pl.BlockSpec((None, 1, LANE), lambda j: (layer, 0, j + off))
    cb = conv_b.reshape(conv_b.shape[0], 1, f2)
    return pl.pallas_call(
        functools.partial(_conv_gate_kernel, rows=rows),
        grid=(nt,),
        in_specs=[col(0), col(nt), wsp(0), wsp(nt), bsp(0), bsp(nt)],
        out_specs=pl.BlockSpec((s_len, LANE), lambda j: (0, j)),
        out_shape=jax.ShapeDtypeStruct((s_len, f), BF16),
        compiler_params=_params(("parallel",)),
        name="conv_gate",
    )(u, u, conv_w, conv_w, cb, cb)


def kernel(x, w_in, sba_norm, hgrn_norm, lb_logits, w_out, pre_mix_norm, post_mix_norm,
           pre_ffn_norm, post_ffn_norm, w_up, conv_w, conv_b, w_down):
    batch, s_len, d_model = x.shape
    assert batch == 1
    depth = w_in.shape[0]
    d_sba = sba_norm.shape[1]
    d_hgrn = hgrn_norm.shape[1]
    n_sba = d_sba // HEAD_DIM
    n_hgrn = d_hgrn // HEAD_DIM
    w_down_b = w_down.astype(BF16)

    xs = x.reshape(s_len, d_model)
    h = norm_cast(xs, pre_mix_norm[0])
    for l in range(depth):
        proj_a = matmul(h, w_in, l, 0, 3 * d_sba, BF16)
        proj_b = matmul(h, w_in, l, 3 * d_sba, 4 * d_hgrn, F32)

        o_a = sba_attention(proj_a, n_sba)
        mixed_a = norm_cast(o_a, sba_norm[l])
        mixed_b = hgrn2(proj_b, lb_logits, hgrn_norm[l], l, n_hgrn)

        y = matmul_split_k(mixed_a, mixed_b, w_out, l)
        xs, h = resid_norm(xs, y, post_mix_norm[l], pre_ffn_norm[l])

        u = matmul(h, w_up, l, 0, w_up.shape[2], F32)
        act = conv_gate(u, conv_w, conv_b, l)
        y = matmul_kgrid(act, w_down_b, l)
        g_next = pre_mix_norm[(l + 1) % depth]
        xs, h = resid_norm(xs, y, post_ffn_norm[l], g_next)
    return xs.reshape(batch, s_len, d_model)
```

```python
import functools

import numpy as np
import jax
import jax.numpy as jnp
from jax import lax
from jax.experimental import pallas as pl
from jax.experimental.pallas import tpu as pltpu

F32 = jnp.float32
BF16 = jnp.bfloat16

HEAD_DIM = 128
EPS = 1e-6
CONV_WIDTH = 3
LOG2E = 1.4426950408889634
LANE = 128
V7X_VMEM_LIMIT = 56 * 1024 * 1024

SBA_SKIP_LOG = -110.0


def _params(sem, vmem=V7X_VMEM_LIMIT):
    return pltpu.CompilerParams(dimension_semantics=sem, vmem_limit_bytes=vmem)


def _rms(x, gain):
    var = jnp.mean(x * x, axis=-1, keepdims=True)
    return x * lax.rsqrt(var + EPS) * gain


def _norm_cast_kernel(x_ref, g_ref, o_ref):
    o_ref[...] = _rms(x_ref[...], g_ref[...]).astype(o_ref.dtype)


def norm_cast(x, gain, tm=256):
    s, d = x.shape
    return pl.pallas_call(
        _norm_cast_kernel,
        grid=(s // tm,),
        in_specs=[pl.BlockSpec((tm, d), lambda i: (i, 0)),
                  pl.BlockSpec((1, d), lambda i: (0, 0))],
        out_specs=pl.BlockSpec((tm, d), lambda i: (i, 0)),
        out_shape=jax.ShapeDtypeStruct((s, d), BF16),
        compiler_params=_params(("parallel",)),
        name="norm_cast",
    )(x, gain.reshape(1, d))


def _resid_norm_kernel(x_ref, y_ref, gpost_ref, gnext_ref, xo_ref, ho_ref):
    xn = x_ref[...] + _rms(y_ref[...], gpost_ref[...])
    xo_ref[...] = xn
    ho_ref[...] = _rms(xn, gnext_ref[...]).astype(ho_ref.dtype)


def resid_norm(x, y, g_post, g_next, tm=256):
    s, d = x.shape
    row = pl.BlockSpec((tm, d), lambda i: (i, 0))
    vec = pl.BlockSpec((1, d), lambda i: (0, 0))
    return pl.pallas_call(
        _resid_norm_kernel,
        grid=(s // tm,),
        in_specs=[row, row, vec, vec],
        out_specs=[row, row],
        out_shape=[jax.ShapeDtypeStruct((s, d), F32), jax.ShapeDtypeStruct((s, d), BF16)],
        compiler_params=_params(("parallel",)),
        name="resid_norm",
    )(x, y, g_post.reshape(1, d), g_next.reshape(1, d))


def _stream_weights(tiles, wf_ref, wb_ref, sem):
    j, i = pl.program_id(0), pl.program_id(1)
    nj, ni = pl.num_programs(0), pl.num_programs(1)
    streams = range(len(tiles))

    def copy(s, jj):
        return pltpu.make_async_copy(tiles[s](jj), wf_ref.at[s], sem.at[s])

    def round_into(slot):
        for s in streams:
            wb_ref[slot, s] = wf_ref[s].astype(BF16)

    @pl.when(jnp.logical_and(j == 0, i == 0))
    def _():
        for s in streams:
            copy(s, 0).start()
        for s in streams:
            copy(s, 0).wait()
        round_into(0)

    @pl.when(jnp.logical_and(i == 0, j + 1 < nj))
    def _():
        for s in streams:
            copy(s, j + 1).start()

    @pl.when(jnp.logical_and(i == ni - 1, j + 1 < nj))
    def _():
        for s in streams:
            copy(s, j + 1).wait()
        round_into((j + 1) % 2)

    return j % 2


def _weight_scratch(n_streams, k, tn):
    return [pltpu.VMEM((n_streams, k, tn), F32), pltpu.VMEM((2, n_streams, k, tn), BF16),
            pltpu.SemaphoreType.DMA((n_streams,))]


_SEQUENTIAL = ("arbitrary", "arbitrary")


def _mm_kernel(a_ref, w_hbm, o_ref, wf_ref, wb_ref, sem, *, layer, joff, tn):
    tile = lambda jj: w_hbm.at[layer, :, pl.ds(pl.multiple_of((jj + joff) * tn, tn), tn)]
    slot = _stream_weights([tile], wf_ref, wb_ref, sem)
    o_ref[...] = jnp.dot(a_ref[...], wb_ref[slot, 0],
                         preferred_element_type=F32).astype(o_ref.dtype)


def matmul(a, w, layer, n_off, n, out_dtype, tm=1024, tn=512):
    m, k = a.shape
    assert m // tm > 1
    return pl.pallas_call(
        functools.partial(_mm_kernel, layer=layer, joff=n_off // tn, tn=tn),
        grid=(n // tn, m // tm),
        in_specs=[pl.BlockSpec((tm, k), lambda j, i: (i, 0)),
                  pl.BlockSpec(memory_space=pl.ANY)],
        out_specs=pl.BlockSpec((tm, tn), lambda j, i: (i, j)),
        out_shape=jax.ShapeDtypeStruct((m, n), out_dtype),
        scratch_shapes=_weight_scratch(1, k, tn),
        compiler_params=_params(_SEQUENTIAL),
        name="matmul",
    )(a, w)


def _mm2_kernel(a_ref, b_ref, w_hbm, o_ref, wf_ref, wb_ref, sem, *, layer, kh, tn):
    tile = lambda r0: (lambda jj: w_hbm.at[layer, pl.ds(r0, kh),
                                           pl.ds(pl.multiple_of(jj * tn, tn), tn)])
    slot = _stream_weights([tile(0), tile(kh)], wf_ref, wb_ref, sem)
    acc = jnp.dot(a_ref[...], wb_ref[slot, 0], preferred_element_type=F32)
    acc += jnp.dot(b_ref[...], wb_ref[slot, 1], preferred_element_type=F32)
    o_ref[...] = acc.astype(o_ref.dtype)


def matmul_split_k(a, b, w, layer, tm=1024, tn=512):
    m, kh = a.shape
    n = w.shape[2]
    assert m // tm > 1
    return pl.pallas_call(
        functools.partial(_mm2_kernel, layer=layer, kh=kh, tn=tn),
        grid=(n // tn, m // tm),
        in_specs=[pl.BlockSpec((tm, kh), lambda j, i: (i, 0)),
                  pl.BlockSpec((tm, kh), lambda j, i: (i, 0)),
                  pl.BlockSpec(memory_space=pl.ANY)],
        out_specs=pl.BlockSpec((tm, tn), lambda j, i: (i, j)),
        out_shape=jax.ShapeDtypeStruct((m, n), F32),
        scratch_shapes=_weight_scratch(2, kh, tn),
        compiler_params=_params(_SEQUENTIAL),
        name="matmul_split_k",
    )(a, b, w)


def _mmk_kernel(a_ref, w_ref, o_ref):
    @pl.when(pl.program_id(2) == 0)
    def _():
        o_ref[...] = jnp.zeros_like(o_ref)
    o_ref[...] += jnp.dot(a_ref[...], w_ref[...], preferred_element_type=F32)


def matmul_kgrid(a, w, layer, tm=1024, tn=512, nk=2):
    m, k = a.shape
    n = w.shape[2]
    tk = k // nk
    return pl.pallas_call(
        _mmk_kernel,
        grid=(n // tn, m // tm, nk),
        in_specs=[pl.BlockSpec((tm, tk), lambda j, i, kk: (i, kk)),
                  pl.BlockSpec((None, tk, tn), lambda j, i, kk: (layer, kk, j))],
        out_specs=pl.BlockSpec((tm, tn), lambda j, i, kk: (i, j)),
        out_shape=jax.ShapeDtypeStruct((m, n), F32),
        compiler_params=_params(("parallel", "parallel", "arbitrary")),
        name="matmul_kgrid",
    )(a, w)


def _softplus_parts(z):
    l = jnp.log(1.0 + jnp.exp(-jnp.abs(z)))
    ls = jnp.minimum(z, 0.0) - l
    return ls, ls - z


def _split_bf16(x):
    hi = x.astype(BF16)
    lo = (x - hi.astype(F32)).astype(BF16)
    return hi, lo


SBA_BAND = 3
SBA_GROUP = 4


def _sba_kernel(q_ref, k_ref, v_ref, uu_ref, o_ref, carry_ref, *, scale):
    s_len = q_ref.shape[0]
    t = LANE
    n_groups = s_len // (t * SBA_GROUP)
    row = lax.broadcasted_iota(jnp.int32, (t, t), 0)
    col = lax.broadcasted_iota(jnp.int32, (t, t), 1)
    diff = col - row
    tri = diff < 0
    nn_t = (((1,), (1,)), ((), ()))

    def suffix_sums(lk):
        hi, lo = _split_bf16(lk)
        cs = jnp.dot(jnp.concatenate([hi, lo], axis=1), uu_ref[...],
                     preferred_element_type=F32)
        return cs[:, :t], cs[:, t:]

    def walk(i, j_start, carry):
        r0 = pl.multiple_of(i * t, t)
        q = q_ref[pl.ds(r0, t), :]

        def cond(st):
            j, cmax, _, _ = st
            return jnp.logical_and(j >= 0, cmax > SBA_SKIP_LOG)

        def body(st):
            j, _, carry, acc = st
            c0 = pl.multiple_of(j * t, t)
            k = k_ref[pl.ds(c0, t), :]
            v = v_ref[pl.ds(c0, t), :]
            z = lax.dot_general(q, k, nn_t, preferred_element_type=F32) * scale
            strict = diff < (r0 - c0)
            ls, lk = _softplus_parts(z)
            suf, tot = suffix_sums(jnp.where(strict, lk, 0.0))
            w = jnp.where(strict, jnp.exp(ls + suf + carry), 0.0)
            acc = acc + jnp.dot(w.astype(BF16), v, preferred_element_type=F32)
            carry = carry + tot
            return j - 1, jnp.max(carry), carry, acc

        init = (j_start, jnp.max(carry), carry, jnp.zeros((t, t), F32))
        return lax.while_loop(cond, body, init)[3]

    def bands(g):
        n_keys = (SBA_GROUP + SBA_BAND - 1) * t
        r0 = pl.multiple_of(g * (SBA_GROUP * t), SBA_GROUP * t)
        c0 = pl.multiple_of(r0 - (SBA_BAND - 1) * t, t)
        q = q_ref[pl.ds(r0, SBA_GROUP * t), :]
        kb = k_ref[pl.ds(c0, n_keys), :]
        vb = v_ref[pl.ds(c0, n_keys), :]
        ls_all, lk_all = [], []
        for u in range(SBA_GROUP):
            z = lax.dot_general(q[u * t:(u + 1) * t, :], kb[u * t:(u + SBA_BAND) * t, :], nn_t,
                                preferred_element_type=F32) * scale
            ls, lk = _softplus_parts(z)
            ls_all.append(ls)
            for b in range(SBA_BAND):
                piece = lk[:, b * t:(b + 1) * t]
                lk_all.append(jnp.where(tri, piece, 0.0) if b == SBA_BAND - 1 else piece)
        suf_all, tot_all = suffix_sums(jnp.concatenate(lk_all, axis=0))
        w_rows, carries = [], []
        for u in range(SBA_GROUP):
            ws = [None] * SBA_BAND
            carry = None
            for b in reversed(range(SBA_BAND)):
                rows = slice((u * SBA_BAND + b) * t, (u * SBA_BAND + b + 1) * t)
                lw = ls_all[u][:, b * t:(b + 1) * t] + suf_all[rows, :]
                if carry is not None:
                    lw = lw + carry
                w = jnp.exp(lw)
                ws[b] = jnp.where(tri, w, 0.0) if b == SBA_BAND - 1 else w
                carry = tot_all[rows, :] if carry is None else carry + tot_all[rows, :]
            zero = jnp.zeros((t, t), F32)
            w_rows.append(jnp.concatenate(
                [zero] * u + ws + [zero] * (SBA_GROUP - 1 - u), axis=1).astype(BF16))
            carries.append(carry)
        out = jnp.dot(jnp.concatenate(w_rows, axis=0), vb, preferred_element_type=F32)
        return out, carries

    for i in range(SBA_GROUP):
        o_ref[i * t:(i + 1) * t, :] = walk(i, i, jnp.zeros((t, t), F32))

    def group(g, _):
        out, carries = bands(g)
        o_ref[pl.ds(pl.multiple_of(g * (SBA_GROUP * t), SBA_GROUP * t), SBA_GROUP * t), :] = out
        cmax = None
        for u in range(SBA_GROUP):
            carry_ref[u * t:(u + 1) * t, :] = carries[u]
            cmax = carries[u] if cmax is None else jnp.maximum(cmax, carries[u])

        @pl.when(jnp.max(cmax) > SBA_SKIP_LOG)
        def _():
            def far(u, _):
                i = g * SBA_GROUP + u
                rows = pl.ds(pl.multiple_of(i * t, t), t)
                carry = carry_ref[pl.ds(pl.multiple_of(u * t, t), t), :]
                o_ref[rows, :] += walk(i, i - SBA_BAND, carry)
                return 0
            lax.fori_loop(0, SBA_GROUP, far, 0)
        return 0

    lax.fori_loop(1, n_groups, group, 0)


def _suffix_matrix():
    t = LANE
    j = np.arange(t)[:, None]
    s = np.arange(t)[None, :]
    u = (j > s).astype(np.float32)
    half = np.concatenate([u, np.ones((t, t), np.float32)], axis=1)
    return jnp.asarray(np.concatenate([half, half], axis=0), dtype=BF16)


def sba_attention(proj_a, n_heads):
    s_len = proj_a.shape[0]
    assert SBA_GROUP >= SBA_BAND - 1 and s_len % (LANE * SBA_GROUP) == 0
    blk = lambda off: pl.BlockSpec((s_len, HEAD_DIM), lambda h: (0, h + off))
    kern = functools.partial(_sba_kernel, scale=HEAD_DIM ** -0.5)
    return pl.pallas_call(
        kern,
        grid=(n_heads,),
        in_specs=[blk(0), blk(n_heads), blk(2 * n_heads),
                  pl.BlockSpec((2 * LANE, 2 * LANE), lambda h: (0, 0))],
        out_specs=pl.BlockSpec((s_len, HEAD_DIM), lambda h: (0, h)),
        out_shape=jax.ShapeDtypeStruct((s_len, n_heads * HEAD_DIM), F32),
        scratch_shapes=[pltpu.VMEM((SBA_GROUP * LANE, LANE), F32)],
        compiler_params=_params(("parallel",)),
        name="sba_attention",
    )(proj_a, proj_a, proj_a, _suffix_matrix())


HGRN_CHUNK = 128
HGRN_UNROLL = 4


def _hgrn_constants(c):
    t = np.arange(c)[:, None]
    r = np.arange(c)[None, :]
    blocks = [(r <= t), (r > t)]
    m = c // 2
    while m >= 1:
        same = (t // m) == (r // m)
        bit = (t & m) != 0
        blocks.append(np.where(bit, same & (r <= t), same & (r > t)))
        m //= 2
    stack = np.concatenate(blocks, axis=0).astype(np.float32)
    x = t ^ r
    lev = np.full((c, c), -2, np.int32)
    lev[np.arange(c), np.arange(c)] = -1
    nlev = len(blocks) - 2
    for i in range(nlev):
        m = c >> (i + 1)
        lev[(r < t) & (x >= m) & (x < 2 * m)] = i
    return jnp.asarray(stack, dtype=BF16), jnp.asarray(lev), nlev


def _sigmoid(x):
    return 1.0 / (1.0 + jnp.exp(-x))


def _hgrn_kernel(q_ref, f_ref, i_ref, g_ref, lbl_ref, gain_ref, stack_ref, lev_ref,
                 o_ref, *, layer, c, nlev):
    s_len = q_ref.shape[0]
    nn_t = (((1,), (1,)), ((), ()))
    tn_n = (((0,), (0,)), ((), ()))

    if layer > 0:
        logits = lbl_ref[...]
        e = jnp.exp(logits - jnp.max(logits, axis=0, keepdims=True))
        p = e / jnp.sum(e, axis=0, keepdims=True)
        lb = p[1:2, :]
        for r in range(2, layer + 1):
            lb = lb + p[r:r + 1, :]
        log_lb = jnp.log(lb)
        log_1m = jnp.log1p(-lb)
        one_m = 1.0 - lb
    gain = gain_ref[...]

    def chunk(ci, st):
        r0 = pl.multiple_of(ci * c, c)
        rows = pl.ds(r0, c)
        zf = f_ref[rows, :]
        e = jnp.exp(-jnp.abs(zf))
        ls = jnp.minimum(zf, 0.0) - jnp.log(1.0 + e)
        sig_neg = jnp.where(zf >= 0.0, e, 1.0) / (1.0 + e)
        if layer > 0:
            b = log_1m + ls
            log_f = jnp.maximum(log_lb, b) + jnp.log(1.0 + jnp.exp(-jnp.abs(log_lb - b)))
            k_in = one_m * sig_neg
        else:
            log_f = ls
            k_in = sig_neg
        qb = q_ref[rows, :]
        q_in = qb * _sigmoid(qb)
        v = i_ref[rows, :].astype(BF16)

        hi, lo = _split_bf16(log_f * LOG2E)
        ex = jnp.dot(stack_ref[...], jnp.concatenate([hi, lo], axis=1),
                     preferred_element_type=F32)
        ex = ex[:, :HEAD_DIM] + ex[:, HEAD_DIM:]
        dec = jnp.exp2(ex).astype(BF16)

        lev = lev_ref[...]
        q_b = q_in.astype(BF16)
        k_b = k_in.astype(BF16)
        a = jnp.where(lev == -1, lax.dot_general(q_b, k_b, nn_t,
                                                 preferred_element_type=F32), 0.0)
        for i in range(nlev):
            d = dec[(2 + i) * c:(3 + i) * c, :]
            a_i = lax.dot_general(q_b * d, k_b * d, nn_t, preferred_element_type=F32)
            a = a + jnp.where(lev == i, a_i, 0.0)

        q_full = q_b * dec[0:c, :]
        k_full = k_b * dec[c:2 * c, :]
        o = jnp.dot(a.astype(BF16), v, preferred_element_type=F32)
        o = o + lax.dot_general(q_full, st.astype(BF16), nn_t, preferred_element_type=F32)
        g_last = ex[c - 1:c, :]
        st = st * jnp.exp2(g_last) + lax.dot_general(v, k_full, tn_n,
                                                     preferred_element_type=F32)

        var = jnp.mean(o * o, axis=-1, keepdims=True)
        gb = g_ref[rows, :]
        o_ref[rows, :] = (o * lax.rsqrt(var + EPS) * gain * (gb * _sigmoid(gb))
                          ).astype(o_ref.dtype)
        return st

    lax.fori_loop(0, s_len // c, chunk, jnp.zeros((HEAD_DIM, HEAD_DIM), F32),
                  unroll=HGRN_UNROLL)


def hgrn2(proj_b, lb_logits, gain, layer, n_heads):
    s_len = proj_b.shape[0]
    depth = lb_logits.shape[0]
    c = HGRN_CHUNK
    stack, lev, nlev = _hgrn_constants(c)
    blk = lambda off: pl.BlockSpec((s_len, HEAD_DIM), lambda h: (0, h + off))
    kern = functools.partial(_hgrn_kernel, layer=layer, c=c, nlev=nlev)
    return pl.pallas_call(
        kern,
        grid=(n_heads,),
        in_specs=[blk(0), blk(n_heads), blk(2 * n_heads), blk(3 * n_heads),
                  pl.BlockSpec((depth, HEAD_DIM), lambda h: (0, h)),
                  pl.BlockSpec((1, HEAD_DIM), lambda h: (0, h)),
                  pl.BlockSpec(stack.shape, lambda h: (0, 0)),
                  pl.BlockSpec(lev.shape, lambda h: (0, 0))],
        out_specs=pl.BlockSpec((s_len, HEAD_DIM), lambda h: (0, h)),
        out_shape=jax.ShapeDtypeStruct((s_len, n_heads * HEAD_DIM), BF16),
        compiler_params=_params(("parallel",)),
        name="hgrn2",
    )(proj_b, proj_b, proj_b, proj_b, lb_logits, gain.reshape(1, -1), stack, lev)


UP_TN = 256


def _up_conv_kernel(a_ref, w_hbm, cwg_ref, cwu_ref, cbg_ref, cbu_ref, o_ref,
                    wf_ref, wb_ref, sem, tg_ref, tu_ref, *, layer, nt, tn):
    tile = lambda off: (lambda jj: w_hbm.at[layer, :,
                                            pl.ds(pl.multiple_of((jj + off) * tn, tn), tn)])
    slot = _stream_weights([tile(0), tile(nt)], wf_ref, wb_ref, sem)

    @pl.when(pl.program_id(1) == 0)
    def _():
        tg_ref[...] = jnp.zeros_like(tg_ref)
        tu_ref[...] = jnp.zeros_like(tu_ref)

    a = a_ref[...]
    tm = a.shape[0]
    r8 = lax.broadcasted_iota(jnp.int32, tg_ref.shape, 0)

    def conv(u, w_ref, b_ref, tail_ref):
        w = w_ref[...]
        b = b_ref[...]

        def taps(cur, m1, m2):
            out = b + w[0:1, :] * m2
            out = out + w[1:2, :] * m1
            return out + w[2:3, :] * cur

        body = taps(u, pltpu.roll(u, 1, 0), pltpu.roll(u, 2, 0))
        tail = tail_ref[...]
        p1 = tail[7:8, :]
        p2 = tail[6:7, :]
        top = u[0:8, :]
        m1 = jnp.where(r8 == 0, p1, pltpu.roll(top, 1, 0))
        m2 = jnp.where(r8 == 0, p2, jnp.where(r8 == 1, p1, pltpu.roll(top, 2, 0)))
        tail_ref[...] = u[tm - 8:tm, :]
        return jnp.concatenate([taps(top, m1, m2), body[8:, :]], axis=0)

    g = conv(jnp.dot(a, wb_ref[slot, 0], preferred_element_type=F32), cwg_ref, cbg_ref, tg_ref)
    u = conv(jnp.dot(a, wb_ref[slot, 1], preferred_element_type=F32), cwu_ref, cbu_ref, tu_ref)
    o_ref[...] = (g * _sigmoid(g) * u).astype(o_ref.dtype)


def up_conv_gate(a, w_up, conv_w, conv_b, layer, tm=1024, tn=UP_TN):
    m, k = a.shape
    f2 = w_up.shape[2]
    f = f2 // 2
    nt = f // tn
    cb = conv_b.reshape(conv_b.shape[0], 1, f2)
    csp = lambda off: pl.BlockSpec((None, CONV_WIDTH, tn), lambda j, i: (layer, 0, j + off))
    bsp = lambda off: pl.BlockSpec((None, 1, tn), lambda j, i: (layer, 0, j + off))
    assert m // tm > 1
    return pl.pallas_call(
        functools.partial(_up_conv_kernel, layer=layer, nt=nt, tn=tn),
        grid=(nt, m // tm),
        in_specs=[pl.BlockSpec((tm, k), lambda j, i: (i, 0)),
                  pl.BlockSpec(memory_space=pl.ANY),
                  csp(0), csp(nt), bsp(0), bsp(nt)],
        out_specs=pl.BlockSpec((tm, tn), lambda j, i: (i, j)),
        out_shape=jax.ShapeDtypeStruct((m, f), BF16),
        scratch_shapes=_weight_scratch(2, k, tn) + [pltpu.VMEM((8, tn), F32),
                                                    pltpu.VMEM((8, tn), F32)],
        compiler_params=_params(_SEQUENTIAL),
        name="up_conv_gate",
    )(a, w_up, conv_w, conv_w, cb, cb)


def kernel(x, w_in, sba_norm, hgrn_norm, lb_logits, w_out, pre_mix_norm, post_mix_norm,
           pre_ffn_norm, post_ffn_norm, w_up, conv_w, conv_b, w_down):
    batch, s_len, d_model = x.shape
    assert batch == 1
    depth = w_in.shape[0]
    d_sba = sba_norm.shape[1]
    d_hgrn = hgrn_norm.shape[1]
    n_sba = d_sba // HEAD_DIM
    n_hgrn = d_hgrn // HEAD_DIM
    w_down_b = w_down.astype(BF16)

    xs = x.reshape(s_len, d_model)
    h = norm_cast(xs, pre_mix_norm[0])
    for l in range(depth):
        proj_a = matmul(h, w_in, l, 0, 3 * d_sba, BF16)
        proj_b = matmul(h, w_in, l, 3 * d_sba, 4 * d_hgrn, F32)

        o_a = sba_attention(proj_a, n_sba)
        mixed_a = norm_cast(o_a, sba_norm[l])
        mixed_b = hgrn2(proj_b, lb_logits, hgrn_norm[l], l, n_hgrn)

        y = matmul_split_k(mixed_a, mixed_b, w_out, l)
        xs, h = resid_norm(xs, y, post_mix_norm[l], pre_ffn_norm[l])

        act = up_conv_gate(h, w_up, conv_w, conv_b, l)
        y = matmul_kgrid(act, w_down_b, l)
        g_next = pre_mix_norm[(l + 1) % depth]
        xs, h = resid_norm(xs, y, post_ffn_norm[l], g_next)
    return xs.reshape(batch, s_len, d_model)
```
